```python
import jax, jax.numpy as jnp
from jax import lax
import numpy as np

D_MODEL = 2048
BATCH = 2
SEQ = 4096
DEPTH = 1
DEC_BATCH = 8
DEC_SEQ = 16
PAST_LEN = 2048

CHUNK = 64
QBLOCK = 128
ROPE_THETA = 500000.0
MLA_HEADS = 16
Q_LORA = 512
KV_LORA = 512
MLA_NOPE = 128
MLA_ROPE = 64
MLA_V = 128
DSA_HEADS = 16
DSA_KV_HEADS = 4
DSA_HEAD_DIM = 128
DSA_ROT = DSA_HEAD_DIM // 4
IDX_HEADS = 16
IDX_DIM = 128
IDX_ROT = IDX_DIM // 4
IDX_SCALE = (IDX_HEADS * IDX_DIM) ** -0.5
TOPK_MAX = 256
D_FF = 4 * D_MODEL
NORM_EPS = 1e-6

kernel_name = "hybrid_mla_dsa_streaming_encoder_step"


def _in_splits():
    return (Q_LORA, KV_LORA, MLA_ROPE,
            DSA_HEADS * DSA_HEAD_DIM, DSA_KV_HEADS * DSA_HEAD_DIM, DSA_KV_HEADS * DSA_HEAD_DIM,
            IDX_HEADS * IDX_DIM, IDX_DIM, IDX_HEADS,
            D_MODEL, D_MODEL)


def _rmsnorm(x, g):
    xf = x.astype(jnp.float32)
    y = xf * lax.rsqrt(jnp.mean(xf * xf, axis=-1, keepdims=True) + NORM_EPS) * g.astype(jnp.float32)
    return y.astype(x.dtype)


def _layernorm(x, g, b):
    xf = x.astype(jnp.float32)
    mu = jnp.mean(xf, axis=-1, keepdims=True)
    var = jnp.mean(jnp.square(xf - mu), axis=-1, keepdims=True)
    y = (xf - mu) * lax.rsqrt(var + NORM_EPS) * g.astype(jnp.float32) + b.astype(jnp.float32)
    return y.astype(x.dtype)


def _rope(x, pos, rot):
    inv = ROPE_THETA ** (-(jnp.arange(0, rot, 2, dtype=jnp.float32) / rot))
    ang = pos.astype(jnp.float32)[:, None] * inv[None, :]
    cos = jnp.cos(ang)[None, :, None, :]
    sin = jnp.sin(ang)[None, :, None, :]
    xr = x[..., :rot].astype(jnp.float32)
    x1, x2 = xr[..., : rot // 2], xr[..., rot // 2:]
    out = jnp.concatenate([x1 * cos - x2 * sin, x2 * cos + x1 * sin], axis=-1).astype(x.dtype)
    return jnp.concatenate([out, x[..., rot:]], axis=-1)


def _map_query_blocks(fn, q_pos, *qs):
    B, Lq = qs[0].shape[:2]
    blk = QBLOCK if Lq % QBLOCK == 0 else Lq
    nb = Lq // blk

    def split(a):
        return jnp.moveaxis(a.reshape((B, nb, blk) + a.shape[2:]), 1, 0)

    out = lax.map(lambda args: fn(*args), (q_pos.reshape(nb, blk),) + tuple(split(a) for a in qs))
    return jnp.moveaxis(out, 0, 1).reshape((B, Lq) + out.shape[3:])


def _mla_attention(q_nope, q_pe, k_nope, k_pe, v, q_pos, k_pos):
    B, Lq = q_nope.shape[:2]
    scale = (MLA_NOPE + MLA_ROPE) ** -0.5
    k_chunk = k_pos // CHUNK

    def block(qp, qn, qr):
        s = jnp.einsum('bqhd,bkhd->bhqk', qn, k_nope) + jnp.einsum('bqhr,bkr->bhqk', qr, k_pe)
        s = s.astype(jnp.float32) * scale
        vis = k_chunk[None, :] <= (qp // CHUNK)[:, None]
        p = jax.nn.softmax(jnp.where(vis[None, None], s, -jnp.inf), axis=-1).astype(v.dtype)
        return jnp.einsum('bhqk,bkhd->bqhd', p, v)

    o = _map_query_blocks(block, q_pos, q_nope, q_pe)
    return o.reshape(B, Lq, MLA_HEADS * MLA_V)


def _dsa_attention(q, q_idx, w_idx, k, v, k_idx, q_pos, k_pos, topk):
    B, Lq = q.shape[:2]
    G = DSA_KV_HEADS
    R = DSA_HEADS // DSA_KV_HEADS
    scale = DSA_HEAD_DIM ** -0.5
    k_chunk = k_pos // CHUNK
    gather = jax.vmap(lambda t, i: t[i])

    def block(qp, qb, qi, wi):
        blk = qp.shape[0]
        q_chunk = qp // CHUNK
        vis = k_chunk[None, :] <= q_chunk[:, None]
        rel = jax.nn.relu(jnp.einsum('bqhd,bkd->bqhk', qi, k_idx).astype(jnp.float32))
        score = jnp.einsum('bqh,bqhk->bqk', wi.astype(jnp.float32), rel)
        score = jnp.where(vis[None], score, -jnp.inf)
        _, sel = lax.top_k(score, topk)
        ok = k_chunk[sel] <= q_chunk[None, :, None]
        k_sel = gather(k, sel)
        v_sel = gather(v, sel)
        qg = qb.reshape(B, blk, G, R, DSA_HEAD_DIM)
        s = jnp.einsum('bqgrd,bqkgd->bqgrk', qg, k_sel).astype(jnp.float32) * scale
        s = jnp.where(ok[:, :, None, None, :], s, -jnp.inf)
        p = jax.nn.softmax(s, axis=-1).astype(v.dtype)
        o = jnp.einsum('bqgrk,bqkgd->bqgrd', p, v_sel)
        return o.reshape(B, blk, DSA_HEADS * DSA_HEAD_DIM)

    return _map_query_blocks(block, q_pos, q, q_idx, w_idx)


def _layer(x, pos, past, w_in, g_q_norm, g_kv_norm, w_uq, w_ukv, w_o_mla, w_o_dsa, w_out,
           ln1_g, ln1_b, w_up, w_down, ln2_g, ln2_b):
    B, L, _ = x.shape
    alpha = (2 * DEPTH) ** 0.25
    proj = jnp.einsum('bld,de->ble', x, w_in)
    cuts = np.cumsum(_in_splits())[:-1].tolist()
    (q_lat, kv_lat, k_pe, q_b, k_b, v_b, q_i, k_i, w_i, gate_a, gate_b) = jnp.split(proj, cuts, axis=-1)
    q = jnp.einsum('blr,rhe->blhe', _rmsnorm(q_lat, g_q_norm), w_uq)
    q_nope = q[..., :MLA_NOPE]
    q_pe = _rope(q[..., MLA_NOPE:], pos, MLA_ROPE)
    c_kv = _rmsnorm(kv_lat, g_kv_norm)
    k_pe = _rope(k_pe[:, :, None, :], pos, MLA_ROPE)[:, :, 0, :]
    q_b = _rope(q_b.reshape(B, L, DSA_HEADS, DSA_HEAD_DIM), pos, DSA_ROT)
    k_b = _rope(k_b.reshape(B, L, DSA_KV_HEADS, DSA_HEAD_DIM), pos, DSA_ROT)
    v_b = v_b.reshape(B, L, DSA_KV_HEADS, DSA_HEAD_DIM)
    q_i = _rope(q_i.reshape(B, L, IDX_HEADS, IDX_DIM), pos, IDX_ROT)
    k_i = _rope(k_i[:, :, None, :], pos, IDX_ROT)[:, :, 0, :]
    w_i = w_i * IDX_SCALE
    new_rows = (c_kv, k_pe, k_b, v_b, k_i)
    if past is None:
        c_kv_all, k_pe_all, k_b_all, v_b_all, k_i_all = new_rows
        k_pos = pos
    else:
        c_kv_all, k_pe_all, k_b_all, v_b_all, k_i_all = tuple(
            jnp.concatenate([p, n], axis=1) for p, n in zip(past, new_rows))
        k_pos = jnp.concatenate([jnp.arange(past[0].shape[1], dtype=jnp.int32), pos])
    Lk = k_pos.shape[0]
    kv = jnp.einsum('bkr,rhe->bkhe', c_kv_all, w_ukv)
    attn_a = _mla_attention(q_nope, q_pe, kv[..., :MLA_NOPE], k_pe_all, kv[..., MLA_NOPE:], pos, k_pos)
    topk = min(TOPK_MAX, Lk // 4)
    attn_b = _dsa_attention(q_b, q_i, w_i, k_b_all, v_b_all, k_i_all, pos, k_pos, topk)
    merged = (jax.nn.sigmoid(gate_a) * jnp.einsum('ble,ed->bld', attn_a, w_o_mla)
              + jax.nn.sigmoid(gate_b) * jnp.einsum('ble,ed->bld', attn_b, w_o_dsa))
    mix = jnp.einsum('bld,de->ble', merged, w_out)
    h = _layernorm(alpha * x + mix, ln1_g, ln1_b)
    f = jnp.einsum('blf,fd->bld', jnp.square(jax.nn.relu(jnp.einsum('bld,df->blf', h, w_up))), w_down)
    y = _layernorm(alpha * h + f, ln2_g, ln2_b)
    return y, new_rows


def setup_inputs(seed: int = 0) -> dict:
    key = jax.random.key(seed)
    ks = iter(jax.random.split(key, 48))
    f32 = jnp.float32

    def nrm(shape, scale):
        return jax.random.normal(next(ks), shape, f32) * scale

    beta = (8 * DEPTH) ** -0.25
    in_scales = (1.0, 1.0, 1.0, 1.0, 1.0, beta, 1.0, 1.0, 1.0, 1.0, 1.0)
    w_in = jnp.concatenate([nrm((DEPTH, D_MODEL, n), D_MODEL ** -0.5 * s)
                            for n, s in zip(_in_splits(), in_scales)], axis=-1)
    x_prompt = nrm((BATCH, SEQ, D_MODEL), 1.0)
    x_sample = nrm((DEC_BATCH, DEC_SEQ, D_MODEL), 1.0)
    cache_mla_latent = nrm((DEPTH, DEC_BATCH, PAST_LEN, KV_LORA), 1.0)
    cache_mla_rope = nrm((DEPTH, DEC_BATCH, PAST_LEN, MLA_ROPE), 1.0)
    cache_dsa_k = nrm((DEPTH, DEC_BATCH, PAST_LEN, DSA_KV_HEADS, DSA_HEAD_DIM), 1.0)
    cache_dsa_v = nrm((DEPTH, DEC_BATCH, PAST_LEN, DSA_KV_HEADS, DSA_HEAD_DIM), beta)
    cache_dsa_idx_k = nrm((DEPTH, DEC_BATCH, PAST_LEN, IDX_DIM), 1.0)
    g_q_norm = 1.0 + nrm((DEPTH, Q_LORA), 0.02)
    g_kv_norm = 1.0 + nrm((DEPTH, KV_LORA), 0.02)
    w_uq = nrm((DEPTH, Q_LORA, MLA_HEADS, MLA_NOPE + MLA_ROPE), Q_LORA ** -0.5)
    w_ukv = jnp.concatenate([nrm((DEPTH, KV_LORA, MLA_HEADS, MLA_NOPE), KV_LORA ** -0.5),
                             nrm((DEPTH, KV_LORA, MLA_HEADS, MLA_V), KV_LORA ** -0.5 * beta)], axis=-1)
    w_o_mla = nrm((DEPTH, MLA_HEADS * MLA_V, D_MODEL), (MLA_HEADS * MLA_V) ** -0.5 * beta)
    w_o_dsa = nrm((DEPTH, DSA_HEADS * DSA_HEAD_DIM, D_MODEL), (DSA_HEADS * DSA_HEAD_DIM) ** -0.5 * beta)
    w_out = nrm((DEPTH, D_MODEL, D_MODEL), D_MODEL ** -0.5 * beta)
    ln1_g = 1.0 + nrm((DEPTH, D_MODEL), 0.02)
    ln1_b = nrm((DEPTH, D_MODEL), 0.02)
    w_up = nrm((DEPTH, D_MODEL, D_FF), D_MODEL ** -0.5)
    w_down = nrm((DEPTH, D_FF, D_MODEL), D_FF ** -0.5 * beta)
    ln2_g = 1.0 + nrm((DEPTH, D_MODEL), 0.02)
    ln2_b = nrm((DEPTH, D_MODEL), 0.02)
    return {"x_prompt": x_prompt, "x_sample": x_sample,
            "cache_mla_latent": cache_mla_latent, "cache_mla_rope": cache_mla_rope,
            "cache_dsa_k": cache_dsa_k, "cache_dsa_v": cache_dsa_v, "cache_dsa_idx_k": cache_dsa_idx_k,
            "w_in": w_in, "g_q_norm": g_q_norm, "g_kv_norm": g_kv_norm, "w_uq": w_uq, "w_ukv": w_ukv,
            "w_o_mla": w_o_mla, "w_o_dsa": w_o_dsa, "w_out": w_out, "ln1_g": ln1_g, "ln1_b": ln1_b,
            "w_up": w_up, "w_down": w_down, "ln2_g": ln2_g, "ln2_b": ln2_b}


def reference(x_prompt, x_sample, cache_mla_latent, cache_mla_rope, cache_dsa_k, cache_dsa_v,
              cache_dsa_idx_k, w_in, g_q_norm, g_kv_norm, w_uq, w_ukv, w_o_mla, w_o_dsa, w_out,
              ln1_g, ln1_b, w_up, w_down, ln2_g, ln2_b):
    params = (w_in, g_q_norm, g_kv_norm, w_uq, w_ukv, w_o_mla, w_o_dsa, w_out,
              ln1_g, ln1_b, w_up, w_down, ln2_g, ln2_b)
    caches = (cache_mla_latent, cache_mla_rope, cache_dsa_k, cache_dsa_v, cache_dsa_idx_k)
    past_len = cache_mla_latent.shape[2]
    pos_p = jnp.arange(x_prompt.shape[1], dtype=jnp.int32)
    pos_s = past_len + jnp.arange(x_sample.shape[1], dtype=jnp.int32)
    hp, hs = x_prompt, x_sample
    rows_p, rows_s = [], []
    for layer in range(DEPTH):
        p_l = tuple(w[layer] for w in params)
        hp, new_p = _layer(hp, pos_p, None, *p_l)
        hs, new_s = _layer(hs, pos_s, tuple(c[layer] for c in caches), *p_l)
        rows_p.append(new_p)
        rows_s.append(new_s)
    p_mla_latent = jnp.stack([r[0] for r in rows_p])
    p_mla_rope = jnp.stack([r[1] for r in rows_p])
    p_dsa_k = jnp.stack([r[2] for r in rows_p])
    p_dsa_v = jnp.stack([r[3] for r in rows_p])
    p_dsa_idx_k = jnp.stack([r[4] for r in rows_p])
    s_mla_latent = jnp.stack([r[0] for r in rows_s])
    s_mla_rope = jnp.stack([r[1] for r in rows_s])
    s_dsa_k = jnp.stack([r[2] for r in rows_s])
    s_dsa_v = jnp.stack([r[3] for r in rows_s])
    s_dsa_idx_k = jnp.stack([r[4] for r in rows_s])
    return (hp, hs, p_mla_latent, p_mla_rope, p_dsa_k, p_dsa_v, p_dsa_idx_k,
            s_mla_latent, s_mla_rope, s_dsa_k, s_dsa_v, s_dsa_idx_k)
```

```python
import functools
import math

import jax
import jax.numpy as jnp
from jax import lax
from jax.experimental import pallas as pl
from jax.experimental.pallas import tpu as pltpu

CHUNK = 64
CHUNK_SHIFT = 6
NEG_INF_KEY = -2139095041
ROPE_THETA = 500000.0
TOPK_MAX = 256
NORM_EPS = 1e-6
MLA_NOPE = 128
MLA_V = 128
LANES = 128
HEAD_PAD = 256
MASK_NEG = -1e30
VMEM_LIMIT = 56 * 1024 * 1024

F32 = jnp.float32
BF16 = jnp.bfloat16
INT_MIN = -(2 ** 31)


def _cparams(sem):
    return pltpu.CompilerParams(dimension_semantics=sem, vmem_limit_bytes=VMEM_LIMIT)


def _dot(a, b):
    return jnp.dot(a, b, preferred_element_type=F32)


def _dot_nt(a, b):
    return lax.dot_general(a, b, (((1,), (1,)), ((), ())), preferred_element_type=F32)


def _pick_tile(n, target, mult):
    if n <= target:
        return n
    t = (target // mult) * mult
    while t >= mult:
        if n % t == 0:
            return t
        t -= mult
    return n


def _rope_tables(pos, rot, period, scale_lanes=None):
    half = rot // 2
    inv = ROPE_THETA ** (-(jnp.arange(0, rot, 2, dtype=F32) / rot))
    ang = pos.astype(F32)[:, None] * inv[None, :]
    cos, sin = jnp.cos(ang), jnp.sin(ang)
    L = pos.shape[0]
    ones = jnp.ones((L, period - rot), F32)
    zeros_h = jnp.zeros((L, half), F32)
    zeros_r = jnp.zeros((L, period - rot), F32)
    c = jnp.concatenate([cos, cos, ones], axis=1)
    a = jnp.concatenate([-sin, zeros_h, zeros_r], axis=1)
    b = jnp.concatenate([zeros_h, sin, zeros_r], axis=1)
    reps = LANES // period
    c, a, b = (jnp.tile(t, (1, reps)) for t in (c, a, b))
    return c, a, b


def _apply_rope(x, c, a, b, shift):
    return x * c + pltpu.roll(x, LANES - shift, 1) * a + pltpu.roll(x, shift, 1) * b


def _proj_kernel(*refs, kind, shift):
    if kind == "rope":
        x_ref, w_ref, c_ref, a_ref, b_ref, o_ref = refs
    elif kind == "rmsnorm":
        x_ref, w_ref, g_ref, o_ref = refs
    else:
        x_ref, w_ref, o_ref = refs
    acc = _dot(x_ref[...], w_ref[...])
    if kind == "rmsnorm":
        ms = jnp.mean(acc * acc, axis=-1, keepdims=True)
        o_ref[...] = (acc * lax.rsqrt(ms + NORM_EPS) * g_ref[...]).astype(o_ref.dtype)
    elif kind == "sigmoid":
        o_ref[...] = jax.nn.sigmoid(acc).astype(o_ref.dtype)
    elif kind == "rope":
        c, a, b = c_ref[...], a_ref[...], b_ref[...]
        for g in range(acc.shape[1] // LANES):
            xg = acc[:, g * LANES:(g + 1) * LANES]
            o_ref[:, g * LANES:(g + 1) * LANES] = _apply_rope(xg, c, a, b, shift).astype(o_ref.dtype)
    else:
        o_ref[...] = acc.astype(o_ref.dtype)


def _proj(x, w, kind, out_dtype, tm, tn, tabs=None, g=None, shift=0, name="proj"):
    M, K = x.shape
    N = w.shape[1]
    in_specs = [pl.BlockSpec((tm, K), lambda i, j: (i, 0)),
                pl.BlockSpec((K, tn), lambda i, j: (0, j))]
    args = [x, w]
    if kind == "rope":
        in_specs += [pl.BlockSpec((tm, LANES), lambda i, j: (i, 0))] * 3
        args += list(tabs)
    elif kind == "rmsnorm":
        in_specs += [pl.BlockSpec((1, tn), lambda i, j: (0, j))]
        args += [g]
    return pl.pallas_call(
        functools.partial(_proj_kernel, kind=kind, shift=shift),
        out_shape=jax.ShapeDtypeStruct((M, N), out_dtype),
        grid=(M // tm, N // tn),
        in_specs=in_specs,
        out_specs=pl.BlockSpec((tm, tn), lambda i, j: (i, j)),
        compiler_params=_cparams(("parallel", "arbitrary")),
        name=name,
    )(*args)


def _proj_heads_kernel(x_ref, w_ref, c_ref, a_ref, b_ref, o_ref, *, shift):
    acc = _dot(x_ref[...], w_ref[...])
    c, a, b = c_ref[...], a_ref[...], b_ref[...]
    for g in range(o_ref.shape[0]):
        xg = acc[:, g * LANES:(g + 1) * LANES]
        o_ref[g] = _apply_rope(xg, c, a, b, shift).astype(o_ref.dtype)


def _proj_heads(x, w, tabs, tm, hb, shift, name):
    M, K = x.shape
    H = w.shape[1] // LANES
    return pl.pallas_call(
        functools.partial(_proj_heads_kernel, shift=shift),
        out_shape=jax.ShapeDtypeStruct((H, M, LANES), BF16),
        grid=(M // tm, H // hb),
        in_specs=[pl.BlockSpec((tm, K), lambda i, j: (i, 0)),
                  pl.BlockSpec((K, hb * LANES), lambda i, j: (0, j)),
                  pl.BlockSpec((tm, LANES), lambda i, j: (i, 0)),
                  pl.BlockSpec((tm, LANES), lambda i, j: (i, 0)),
                  pl.BlockSpec((tm, LANES), lambda i, j: (i, 0))],
        out_specs=pl.BlockSpec((hb, tm, LANES), lambda i, j: (j, i, 0)),
        compiler_params=_cparams(("parallel", "arbitrary")),
        name=name,
    )(x, w, *tabs)


def _q_up_kernel(x_ref, w_ref, c_ref, a_ref, b_ref, o_ref, *, shift):
    x = x_ref[...].astype(BF16)
    c, a, b = c_ref[...], a_ref[...], b_ref[...]
    for h in range(o_ref.shape[0]):
        acc = _dot(x, w_ref[:, h * HEAD_PAD:(h + 1) * HEAD_PAD])
        o_ref[h, :, :LANES] = acc[:, :LANES].astype(o_ref.dtype)
        o_ref[h, :, LANES:] = _apply_rope(acc[:, LANES:], c, a, b, shift).astype(o_ref.dtype)


def _q_up(qn, R, w, tabs, tm, shift, name):
    M = qn.shape[0]
    H = w.shape[1] // HEAD_PAD
    return pl.pallas_call(
        functools.partial(_q_up_kernel, shift=shift),
        out_shape=jax.ShapeDtypeStruct((H, M, HEAD_PAD), BF16),
        grid=(M // tm,),
        in_specs=[pl.BlockSpec((tm, R), lambda i: (i, 0)),
                  pl.BlockSpec((R, H * HEAD_PAD), lambda i: (0, 0)),
                  pl.BlockSpec((tm, LANES), lambda i: (i, 0)),
                  pl.BlockSpec((tm, LANES), lambda i: (i, 0)),
                  pl.BlockSpec((tm, LANES), lambda i: (i, 0))],
        out_specs=pl.BlockSpec((H, tm, HEAD_PAD), lambda i: (0, i, 0)),
        compiler_params=_cparams(("parallel",)),
        name=name,
    )(qn, w, *tabs)


def _kv_up_kernel(c_ref, pe_ref, wk_ref, wv_ref, k_ref, v_ref):
    c = c_ref[...]
    pe = pe_ref[...]
    for h in range(k_ref.shape[0]):
        k_ref[h, :, :LANES] = _dot(c, wk_ref[:, h * LANES:(h + 1) * LANES]).astype(k_ref.dtype)
        k_ref[h, :, LANES:] = pe
        v_ref[h] = _dot(c, wv_ref[:, h * LANES:(h + 1) * LANES]).astype(v_ref.dtype)


def _kv_up(c2, pe2, wk, wv, tl, name):
    R_, R = c2.shape
    H = wk.shape[1] // LANES
    return pl.pallas_call(
        _kv_up_kernel,
        out_shape=(jax.ShapeDtypeStruct((H, R_, HEAD_PAD), BF16),
                   jax.ShapeDtypeStruct((H, R_, LANES), BF16)),
        grid=(R_ // tl,),
        in_specs=[pl.BlockSpec((tl, R), lambda i: (i, 0)),
                  pl.BlockSpec((tl, LANES), lambda i: (i, 0)),
                  pl.BlockSpec((R, H * LANES), lambda i: (0, 0)),
                  pl.BlockSpec((R, H * LANES), lambda i: (0, 0))],
        out_specs=(pl.BlockSpec((H, tl, HEAD_PAD), lambda i: (0, i, 0)),
                   pl.BlockSpec((H, tl, LANES), lambda i: (0, i, 0))),
        compiler_params=_cparams(("parallel",)),
        name=name,
    )(c2, pe2, wk, wv)


def _visible_limit(qpos):
    return ((qpos >> CHUNK_SHIFT) + 1) << CHUNK_SHIFT


def _vis_mask(qpos0, tq, k0, tk, kv_len):
    qp = qpos0 + lax.broadcasted_iota(jnp.int32, (tq, tk), 0)
    kp = k0 + lax.broadcasted_iota(jnp.int32, (tq, tk), 1)
    lim = jnp.minimum(_visible_limit(qp), kv_len)
    return kp < lim


def _softmax_step(s, v, m_ref, l_ref, acc_ref):
    m_old = m_ref[...]
    m_new = jnp.maximum(m_old, jnp.max(s, axis=-1, keepdims=True))
    alpha = jnp.exp(m_old - m_new)
    p = jnp.exp(s - m_new)
    l_ref[...] = alpha * l_ref[...] + jnp.sum(p, axis=-1, keepdims=True)
    acc_ref[...] = alpha * acc_ref[...] + _dot(p.astype(v.dtype), v)
    m_ref[...] = m_new


def _mla_kernel(q_ref, k_ref, v_ref, o_ref, m_ref, l_ref, acc_ref, *, tq, tk, q_off, kv_len, scale):
    qi = pl.program_id(2)
    qpos0 = q_off + qi * tq
    q = q_ref[0, 0]
    m_ref[...] = jnp.full(m_ref.shape, MASK_NEG, F32)
    l_ref[...] = jnp.zeros(l_ref.shape, F32)
    acc_ref[...] = jnp.zeros(acc_ref.shape, F32)

    n_full = jnp.minimum(_visible_limit(qpos0), kv_len) // tk
    n_vis = (jnp.minimum(_visible_limit(qpos0 + tq - 1), kv_len) + tk - 1) // tk

    def tile(j, masked):
        k0 = pl.multiple_of(j * tk, tk)
        k = k_ref[0, 0, pl.ds(k0, tk), :]
        v = v_ref[0, 0, pl.ds(k0, tk), :]
        s = _dot_nt(q, k) * scale
        if masked:
            s = jnp.where(_vis_mask(qpos0, tq, k0, tk, kv_len), s, MASK_NEG)
        _softmax_step(s, v, m_ref, l_ref, acc_ref)

    def full_body(j, carry):
        tile(j, False)
        return carry

    def edge_body(j, carry):
        tile(j, True)
        return carry

    lax.fori_loop(0, n_full, full_body, 0)
    lax.fori_loop(n_full, n_vis, edge_body, 0)
    o_ref[0] = (acc_ref[...] / l_ref[...]).astype(o_ref.dtype)


def _mla_attn(q, k, v, tq, tk, q_off, kv_len, qk_dim, name):
    H, B, Lq, _ = q.shape
    Lk = k.shape[2]
    scale = qk_dim ** -0.5
    return pl.pallas_call(
        functools.partial(_mla_kernel, tq=tq, tk=tk, q_off=q_off, kv_len=kv_len, scale=scale),
        out_shape=jax.ShapeDtypeStruct((B, Lq, H * MLA_V), BF16),
        grid=(B, H, Lq // tq),
        in_specs=[pl.BlockSpec((1, 1, tq, HEAD_PAD), lambda b, h, i: (h, b, i, 0)),
                  pl.BlockSpec((1, 1, Lk, HEAD_PAD), lambda b, h, i: (h, b, 0, 0)),
                  pl.BlockSpec((1, 1, Lk, MLA_V), lambda b, h, i: (h, b, 0, 0))],
        out_specs=pl.BlockSpec((1, tq, MLA_V), lambda b, h, i: (b, i, h)),
        scratch_shapes=[pltpu.VMEM((tq, 1), F32), pltpu.VMEM((tq, 1), F32), pltpu.VMEM((tq, MLA_V), F32)],
        compiler_params=_cparams(("parallel", "parallel", "arbitrary")),
        name=name,
    )(q, k, v)


def _sortable_key(x):
    i = pltpu.bitcast(x + 0.0, jnp.int32)
    return i ^ ((i >> 31) & 0x7FFFFFFF)


def _dsa_kernel(qi_ref, w_ref, ki_ref, q_ref, k_ref, v_ref, o_ref,
                key_ref, bias_ref, m_ref, l_ref, acc_ref,
                *, tq, tk, q_off, kv_len, topk, n_idx_heads, n_groups, rep, w_lane0, scale):
    i_q = pl.program_id(1)
    qpos0 = q_off + i_q * tq
    Lk = key_ref.shape[1]
    n_vis = (jnp.minimum(_visible_limit(qpos0 + tq - 1), kv_len) + tk - 1) // tk
    neg_key = NEG_INF_KEY

    q_idx = qi_ref[:, 0].reshape(n_idx_heads * tq, LANES)
    w = w_ref[0]

    def score_body(j, carry):
        k0 = pl.multiple_of(j * tk, tk)
        kt = ki_ref[0, pl.ds(k0, tk), :]
        r = _dot_nt(q_idx, kt)
        sc = jnp.zeros((tq, tk), F32)
        for h in range(n_idx_heads):
            sc = sc + w[:, w_lane0 + h:w_lane0 + h + 1] * jnp.maximum(r[h * tq:(h + 1) * tq], 0.0)
        sc = jnp.where(_vis_mask(qpos0, tq, k0, tk, kv_len), sc, -jnp.inf)
        key_ref[:, pl.ds(k0, tk)] = _sortable_key(sc)
        return carry

    lax.fori_loop(0, n_vis, score_body, 0)

    def count(pred_fn):
        def body(j, acc):
            k0 = pl.multiple_of(j * tk, tk)
            keys = key_ref[:, pl.ds(k0, tk)]
            idx = k0 + lax.broadcasted_iota(jnp.int32, (tq, tk), 1)
            hit = pred_fn(keys, idx).astype(jnp.int32)
            for g in range(tk // LANES):
                acc = acc + hit[:, g * LANES:(g + 1) * LANES]
            return acc
        acc = lax.fori_loop(0, n_vis, body, jnp.zeros((tq, LANES), jnp.int32))
        return jnp.sum(acc, axis=-1, keepdims=True)

    def bit_body(bi, t):
        cand = t | lax.shift_left(jnp.int32(1), 31 - bi)
        cs = cand ^ INT_MIN
        cnt = count(lambda keys, idx: keys >= cs)
        return jnp.where(cnt >= topk, cand, t)

    t_u = lax.fori_loop(0, 32, bit_body, jnp.zeros((tq, 1), jnp.int32))
    vk = t_u ^ INT_MIN
    c_gt = count(lambda keys, idx: keys > vk)
    c_eq = count(lambda keys, idx: keys == vk)
    need = topk - c_gt

    idx_bits = max(1, (Lk - 1).bit_length())
    tie_rows = jnp.logical_and(c_eq > need, vk > neg_key)
    any_tie = jnp.max(tie_rows.astype(jnp.int32)) > 0

    def tie_search():
        def jbit(bi, jv):
            cand = jv | lax.shift_left(jnp.int32(1), idx_bits - 1 - bi)
            cnt = count(lambda keys, idx: jnp.logical_and(keys == vk, idx < cand))
            return jnp.where(cnt < need, cand, jv)
        return lax.fori_loop(0, idx_bits, jbit, jnp.zeros((tq, 1), jnp.int32))

    j_lim = lax.cond(any_tie, tie_search, lambda: jnp.full((tq, 1), 2 ** 30, jnp.int32))

    def bias_body(j, carry):
        k0 = pl.multiple_of(j * tk, tk)
        keys = key_ref[:, pl.ds(k0, tk)]
        idx = k0 + lax.broadcasted_iota(jnp.int32, (tq, tk), 1)
        sel = jnp.logical_or(keys > vk, jnp.logical_and(keys == vk, idx <= j_lim))
        sel = jnp.logical_and(sel, keys > neg_key)
        bias_ref[:, pl.ds(k0, tk)] = jnp.where(sel, 0.0, MASK_NEG).astype(F32)
        return carry

    lax.fori_loop(0, n_vis, bias_body, 0)

    for g in range(n_groups):
        qg = q_ref[g * rep:(g + 1) * rep, 0].reshape(rep * tq, LANES)
        m_ref[...] = jnp.full(m_ref.shape, MASK_NEG, F32)
        l_ref[...] = jnp.zeros(l_ref.shape, F32)
        acc_ref[...] = jnp.zeros(acc_ref.shape, F32)

        def att_body(j, carry):
            k0 = pl.multiple_of(j * tk, tk)
            kt = k_ref[0, pl.ds(k0, tk), g * LANES:(g + 1) * LANES]
            vt = v_ref[0, pl.ds(k0, tk), g * LANES:(g + 1) * LANES]
            s = _dot_nt(qg, kt) * scale
            s = (s.reshape(rep, tq, tk) + bias_ref[:, pl.ds(k0, tk)][None]).reshape(rep * tq, tk)
            _softmax_step(s, vt, m_ref, l_ref, acc_ref)
            return carry

        lax.fori_loop(0, n_vis, att_body, 0)
        out = acc_ref[...] / l_ref[...]
        for r_ in range(rep):
            h = g * rep + r_
            o_ref[0, :, h * LANES:(h + 1) * LANES] = out[r_ * tq:(r_ + 1) * tq].astype(o_ref.dtype)


def _dsa_attn(qheads, n_heads, w_tail, k_idx, k, v, tq, tk, q_off, kv_len, topk, w_lane0, name):
    _, B, Lq, _ = qheads.shape
    H = n_heads
    Hi = qheads.shape[0] - H
    assert H == Hi, "attention and indexer head blocks are addressed as equal halves"
    Lk = k.shape[1]
    G = k.shape[2] // LANES
    rep = H // G
    kern = functools.partial(
        _dsa_kernel, tq=tq, tk=tk, q_off=q_off, kv_len=kv_len, topk=topk, n_idx_heads=Hi,
        n_groups=G, rep=rep, w_lane0=w_lane0, scale=LANES ** -0.5)
    return pl.pallas_call(
        kern,
        out_shape=jax.ShapeDtypeStruct((B, Lq, H * LANES), BF16),
        grid=(B, Lq // tq),
        in_specs=[pl.BlockSpec((Hi, 1, tq, LANES), lambda b, i: (1, b, i, 0)),
                  pl.BlockSpec((1, tq, LANES), lambda b, i: (b, i, 0)),
                  pl.BlockSpec((1, Lk, LANES), lambda b, i: (b, 0, 0)),
                  pl.BlockSpec((H, 1, tq, LANES), lambda b, i: (0, b, i, 0)),
                  pl.BlockSpec((1, Lk, G * LANES), lambda b, i: (b, 0, 0)),
                  pl.BlockSpec((1, Lk, G * LANES), lambda b, i: (b, 0, 0))],
        out_specs=pl.BlockSpec((1, tq, H * LANES), lambda b, i: (b, i, 0)),
        scratch_shapes=[pltpu.VMEM((tq, Lk), jnp.int32), pltpu.VMEM((tq, Lk), F32),
                        pltpu.VMEM((rep * tq, 1), F32), pltpu.VMEM((rep * tq, 1), F32),
                        pltpu.VMEM((rep * tq, LANES), F32)],
        compiler_params=_cparams(("parallel", "arbitrary")),
        name=name,
    )(qheads, w_tail, k_idx, qheads, k, v)


def _merge_kernel(a_ref, b_ref, ga_ref, gb_ref, wa_ref, wb_ref, o_ref):
    ya = _dot(a_ref[...], wa_ref[...])
    yb = _dot(b_ref[...], wb_ref[...])
    o_ref[...] = (ga_ref[...].astype(F32) * ya + gb_ref[...].astype(F32) * yb).astype(o_ref.dtype)


def _merge(attn_a, attn_b, gates, wa, wb, tm, tn, name):
    M, Ka = attn_a.shape
    Kb = attn_b.shape[1]
    N = wa.shape[1]
    nb = N // tn
    return pl.pallas_call(
        _merge_kernel,
        out_shape=jax.ShapeDtypeStruct((M, N), BF16),
        grid=(M // tm, nb),
        in_specs=[pl.BlockSpec((tm, Ka), lambda i, j: (i, 0)),
                  pl.BlockSpec((tm, Kb), lambda i, j: (i, 0)),
                  pl.BlockSpec((tm, tn), lambda i, j: (i, j)),
                  pl.BlockSpec((tm, tn), lambda i, j: (i, j + nb)),
                  pl.BlockSpec((Ka, tn), lambda i, j: (0, j)),
                  pl.BlockSpec((Kb, tn), lambda i, j: (0, j))],
        out_specs=pl.BlockSpec((tm, tn), lambda i, j: (i, j)),
        compiler_params=_cparams(("parallel", "arbitrary")),
        name=name,
    )(attn_a, attn_b, gates, gates, wa, wb)


def _layernorm(z, g, b):
    mu = jnp.mean(z, axis=-1, keepdims=True)
    zc = z - mu
    var = jnp.mean(zc * zc, axis=-1, keepdims=True)
    return zc * lax.rsqrt(var + NORM_EPS) * g + b


def _out_ln_kernel(mg_ref, x_ref, w_ref, g_ref, b_ref, o_ref, *, alpha):
    mix = _dot(mg_ref[...], w_ref[...])
    o_ref[...] = _layernorm(alpha * x_ref[...] + mix, g_ref[...], b_ref[...])


def _out_ln(merged, x, w, g, b, tm, alpha, name):
    M, D = x.shape
    return pl.pallas_call(
        functools.partial(_out_ln_kernel, alpha=alpha),
        out_shape=jax.ShapeDtypeStruct((M, D), F32),
        grid=(M // tm,),
        in_specs=[pl.BlockSpec((tm, D), lambda i: (i, 0)),
                  pl.BlockSpec((tm, D), lambda i: (i, 0)),
                  pl.BlockSpec((D, D), lambda i: (0, 0)),
                  pl.BlockSpec((1, D), lambda i: (0, 0)),
                  pl.BlockSpec((1, D), lambda i: (0, 0))],
        out_specs=pl.BlockSpec((tm, D), lambda i: (i, 0)),
        compiler_params=_cparams(("parallel",)),
        name=name,
    )(merged, x, w, g, b)


def _ffn_kernel(h_ref, wu_ref, wd_ref, g_ref, b_ref, o_ref, acc_ref, *, alpha):
    f = pl.program_id(1)

    @pl.when(f == 0)
    def _():
        acc_ref[...] = jnp.zeros(acc_ref.shape, F32)

    u = jnp.maximum(_dot(h_ref[...].astype(BF16), wu_ref[...]), 0.0)
    acc_ref[...] += _dot((u * u).astype(BF16), wd_ref[...])

    @pl.when(f == pl.num_programs(1) - 1)
    def _():
        o_ref[...] = _layernorm(alpha * h_ref[...] + acc_ref[...], g_ref[...], b_ref[...])


def _ffn(h, wu, wd, g, b, tm, tf, alpha, name):
    M, D = h.shape
    F = wu.shape[1]
    return pl.pallas_call(
        functools.partial(_ffn_kernel, alpha=alpha),
        out_shape=jax.ShapeDtypeStruct((M, D), F32),
        grid=(M // tm, F // tf),
        in_specs=[pl.BlockSpec((tm, D), lambda i, f: (i, 0)),
                  pl.BlockSpec((D, tf), lambda i, f: (0, f)),
                  pl.BlockSpec((tf, D), lambda i, f: (f, 0)),
                  pl.BlockSpec((1, D), lambda i, f: (0, 0)),
                  pl.BlockSpec((1, D), lambda i, f: (0, 0))],
        out_specs=pl.BlockSpec((tm, D), lambda i, f: (i, 0)),
        scratch_shapes=[pltpu.VMEM((tm, D), F32)],
        compiler_params=_cparams(("parallel", "arbitrary")),
        name=name,
    )(h, wu, wd, g, b)


def _prep_weights(w_in, g_q_norm, g_kv_norm, w_uq, w_ukv, w_o_mla, w_o_dsa, w_out, w_up, w_down, dims):
    (q_lora, kv_lora, mla_rope, dsa_q, dsa_kv, idx_q, idx_dim, idx_heads, d_model) = dims
    cuts = [q_lora, kv_lora, mla_rope, dsa_q, dsa_kv, dsa_kv, idx_q, idx_dim, idx_heads, d_model, d_model]
    offs = [0]
    for c in cuts:
        offs.append(offs[-1] + c)
    col = lambda i: w_in[:, offs[i]:offs[i + 1]]
    wb = lambda t: t.astype(BF16)
    tail_pad = LANES - mla_rope - idx_heads
    W = {
        "lat": wb(jnp.concatenate([col(0), col(1)], axis=1)),
        "qheads": wb(jnp.concatenate([col(3), col(6)], axis=1)),
        "krope": wb(jnp.concatenate([col(4), col(7)], axis=1)),
        "v": wb(col(5)),
        "gates": wb(jnp.concatenate([col(9), col(10)], axis=1)),
        "tail": wb(jnp.concatenate([col(2), col(8), jnp.zeros((w_in.shape[0], tail_pad), w_in.dtype)], axis=1)),
        "g_lat": jnp.concatenate([g_q_norm, g_kv_norm])[None, :].astype(F32),
    }
    H = w_uq.shape[1]
    pe = w_uq.shape[2] - MLA_NOPE
    wq = jnp.concatenate([w_uq, jnp.zeros((w_uq.shape[0], H, HEAD_PAD - MLA_NOPE - pe), w_uq.dtype)], axis=2)
    W["uq"] = wb(wq.reshape(w_uq.shape[0], H * HEAD_PAD))
    W["uk"] = wb(w_ukv[:, :, :MLA_NOPE].reshape(w_ukv.shape[0], H * MLA_NOPE))
    W["uv"] = wb(w_ukv[:, :, MLA_NOPE:].reshape(w_ukv.shape[0], H * MLA_V))
    W["o_mla"], W["o_dsa"], W["out"] = wb(w_o_mla), wb(w_o_dsa), wb(w_out)
    W["up"], W["down"] = wb(w_up), wb(w_down)
    return W


def _layer(x, q_off, past, W, ln, dims, alpha, tag):
    (q_lora, kv_lora, mla_rope, dsa_q, dsa_kv, idx_q, idx_dim, idx_heads, d_model) = dims
    B, L, D = x.shape
    M = B * L
    assert q_lora == kv_lora and mla_rope + idx_heads <= LANES
    x2 = x.reshape(M, D)
    xb = x2.astype(BF16)
    pos = q_off + jnp.arange(L, dtype=jnp.int32)
    tm = _pick_tile(M, 1024, 16)

    lane = jnp.arange(LANES)[None, :]
    t32 = tuple(jnp.tile(t, (B, 1)) for t in _rope_tables(pos, LANES // 4, LANES))
    t64 = tuple(jnp.tile(jnp.where(lane < mla_rope, t, 0.0).astype(F32), (B, 1))
                for t in _rope_tables(pos, mla_rope, mla_rope))
    idx_scale = float(idx_q) ** -0.5
    in_w = jnp.logical_and(lane >= mla_rope, lane < mla_rope + idx_heads)
    tail_tabs = (jnp.where(in_w, idx_scale, t64[0]).astype(F32), t64[1], t64[2])

    lat = _proj(xb, W["lat"], "rmsnorm", F32, tm, q_lora, g=W["g_lat"], name=f"proj_lat_{tag}")
    qheads = _proj_heads(xb, W["qheads"], t32, tm, 4, LANES // 8, name=f"proj_qheads_{tag}")
    krope = _proj(xb, W["krope"], "rope", F32, tm, W["krope"].shape[1], tabs=t32,
                  shift=LANES // 8, name=f"proj_krope_{tag}")
    v_b = _proj(xb, W["v"], "none", F32, tm, W["v"].shape[1], name=f"proj_v_{tag}")
    gates = _proj(xb, W["gates"], "sigmoid", F32, tm, _pick_tile(d_model, 512, LANES), name=f"proj_gates_{tag}")
    tail = _proj(xb, W["tail"], "rope", F32, tm, LANES, tabs=tail_tabs, shift=mla_rope // 2,
                 name=f"proj_tail_{tag}")

    c_kv = lat[:, q_lora:].reshape(B, L, kv_lora)
    k_pe = tail[:, :mla_rope].reshape(B, L, mla_rope)
    k_b = krope[:, :dsa_kv].reshape(B, L, dsa_kv)
    k_i = krope[:, dsa_kv:].reshape(B, L, idx_dim)
    v_b3 = v_b.reshape(B, L, dsa_kv)
    new_rows = (c_kv, k_pe, k_b, v_b3, k_i)

    tail3 = tail.reshape(B, L, LANES)
    pe_new = jnp.where(lane[None] < mla_rope, tail3, 0.0)
    if past is None:
        kv_len = L
        tk = _pick_tile(L, 512, LANES)
        Lk = L
        c_all, pe_all, kb_all, vb_all, ki_all = (t.astype(BF16) for t in (c_kv, pe_new, k_b, v_b3, k_i))
    else:
        p_lat, p_rope, p_k, p_v, p_ik = past
        P = p_lat.shape[1]
        kv_len = P + L
        tk = 256
        Lk = -(-kv_len // tk) * tk
        p_rope_pad = jnp.concatenate([p_rope, jnp.zeros((B, P, LANES - mla_rope), p_rope.dtype)], axis=2)

        def cat(p, n):
            z = jnp.zeros((B, Lk - kv_len, n.shape[2]), BF16)
            return jnp.concatenate([p.astype(BF16), n.astype(BF16), z], axis=1)

        c_all, pe_all = cat(p_lat, c_kv), cat(p_rope_pad, pe_new)
        kb_all, vb_all = cat(p_k.reshape(B, P, dsa_kv), k_b), cat(p_v.reshape(B, P, dsa_kv), v_b3)
        ki_all = cat(p_ik, k_i)

    H = W["uk"].shape[1] // MLA_NOPE
    q_cat = _q_up(lat, q_lora, W["uq"], t64, tm, mla_rope // 2, name=f"q_up_{tag}")
    k_cat, v_mla = _kv_up(c_all.reshape(B * Lk, kv_lora), pe_all.reshape(B * Lk, LANES), W["uk"], W["uv"],
                          _pick_tile(B * Lk, 512, 16), name=f"kv_up_{tag}")
    tq_a = _pick_tile(L, 512, 16)
    attn_a = _mla_attn(q_cat.reshape(H, B, L, HEAD_PAD), k_cat.reshape(H, B, Lk, HEAD_PAD),
                       v_mla.reshape(H, B, Lk, MLA_V), tq_a, tk, q_off, kv_len, MLA_NOPE + mla_rope,
                       name=f"mla_attn_{tag}")

    n_dsa_heads = dsa_q // LANES
    topk = min(TOPK_MAX, kv_len // 4)
    tq_b = _pick_tile(L, 128, 16)
    attn_b = _dsa_attn(qheads.reshape(-1, B, L, LANES), n_dsa_heads, tail3, ki_all, kb_all, vb_all,
                       tq_b, tk, q_off, kv_len, topk, mla_rope, name=f"dsa_attn_{tag}")

    tn = _pick_tile(d_model, 512, LANES)
    merged = _merge(attn_a.reshape(M, -1), attn_b.reshape(M, -1), gates, W["o_mla"], W["o_dsa"],
                    tm, tn, name=f"merge_{tag}")
    tm_s = _pick_tile(M, 512, 16)
    h = _out_ln(merged, x2, W["out"], ln[0], ln[1], tm_s, alpha, name=f"out_ln_{tag}")
    y = _ffn(h, W["up"], W["down"], ln[2], ln[3], tm_s, _pick_tile(W["up"].shape[1], 512, LANES), alpha,
             name=f"ffn_{tag}")
    return y.reshape(B, L, D), new_rows


def kernel(x_prompt, x_sample, cache_mla_latent, cache_mla_rope, cache_dsa_k, cache_dsa_v, cache_dsa_idx_k, w_in, g_q_norm, g_kv_norm, w_uq, w_ukv, w_o_mla, w_o_dsa, w_out, ln1_g, ln1_b, w_up, w_down, ln2_g, ln2_b):
    depth = w_in.shape[0]
    alpha = (2 * depth) ** 0.25
    d_model = x_prompt.shape[2]
    q_lora = g_q_norm.shape[1]
    kv_lora = g_kv_norm.shape[1]
    mla_rope = cache_mla_rope.shape[3]
    G, dh = cache_dsa_k.shape[3], cache_dsa_k.shape[4]
    dsa_kv = G * dh
    dsa_q = w_o_dsa.shape[1]
    idx_dim = cache_dsa_idx_k.shape[3]
    fixed = q_lora + kv_lora + mla_rope + dsa_q + 2 * dsa_kv + idx_dim + 2 * d_model
    idx_heads = (w_in.shape[2] - fixed) // (idx_dim + 1)
    idx_q = idx_heads * idx_dim
    dims = (q_lora, kv_lora, mla_rope, dsa_q, dsa_kv, idx_q, idx_dim, idx_heads, d_model)
    past_len = cache_mla_latent.shape[2]

    hp, hs = x_prompt, x_sample
    rows_p, rows_s = [], []
    for layer in range(depth):
        W = _prep_weights(w_in[layer], g_q_norm[layer], g_kv_norm[layer], w_uq[layer], w_ukv[layer],
                          w_o_mla[layer], w_o_dsa[layer], w_out[layer], w_up[layer], w_down[layer], dims)
        ln = tuple(t[layer][None, :].astype(F32) for t in (ln1_g, ln1_b, ln2_g, ln2_b))
        past = (cache_mla_latent[layer], cache_mla_rope[layer], cache_dsa_k[layer], cache_dsa_v[layer],
                cache_dsa_idx_k[layer])
        hp, new_p = _layer(hp, 0, None, W, ln, dims, alpha, f"p{layer}")
        hs, new_s = _layer(hs, past_len, past, W, ln, dims, alpha, f"s{layer}")
        rows_p.append(new_p)
        rows_s.append(new_s)

    def stack(rows, i, shape4=None):
        t = jnp.stack([r[i] for r in rows])
        if shape4 is not None:
            t = t.reshape(t.shape[:3] + shape4)
        return t

    outs = [hp, hs]
    for rows in (rows_p, rows_s):
        outs += [stack(rows, 0), stack(rows, 1), stack(rows, 2, (G, dh)), stack(rows, 3, (G, dh)), stack(rows, 4)]
    return tuple(outs)
```

```python
import functools
import math

import jax
import jax.numpy as jnp
from jax import lax
from jax.experimental import pallas as pl
from jax.experimental.pallas import tpu as pltpu

CHUNK = 64
CHUNK_SHIFT = 6
NEG_INF_KEY = -2139095041
ROPE_THETA = 500000.0
TOPK_MAX = 256
NORM_EPS = 1e-6
MLA_NOPE = 128
MLA_V = 128
LANES = 128
HEAD_PAD = 256
MASK_NEG = -1e30
LOG2E = math.log2(math.e)
VMEM_LIMIT = 56 * 1024 * 1024

F32 = jnp.float32
BF16 = jnp.bfloat16
INT_MIN = -(2 ** 31)


def _cparams(sem):
    return pltpu.CompilerParams(dimension_semantics=sem, vmem_limit_bytes=VMEM_LIMIT)


def _dot(a, b):
    return jnp.dot(a, b, preferred_element_type=F32)


def _dot_nt(a, b):
    return lax.dot_general(a, b, (((1,), (1,)), ((), ())), preferred_element_type=F32)


def _pick_tile(n, target, mult):
    if n <= target:
        return n
    t = (target // mult) * mult
    while t >= mult:
        if n % t == 0:
            return t
        t -= mult
    return n


def _rope_tables(pos, rot, period, scale_lanes=None):
    half = rot // 2
    inv = ROPE_THETA ** (-(jnp.arange(0, rot, 2, dtype=F32) / rot))
    ang = pos.astype(F32)[:, None] * inv[None, :]
    cos, sin = jnp.cos(ang), jnp.sin(ang)
    L = pos.shape[0]
    ones = jnp.ones((L, period - rot), F32)
    zeros_h = jnp.zeros((L, half), F32)
    zeros_r = jnp.zeros((L, period - rot), F32)
    c = jnp.concatenate([cos, cos, ones], axis=1)
    a = jnp.concatenate([-sin, zeros_h, zeros_r], axis=1)
    b = jnp.concatenate([zeros_h, sin, zeros_r], axis=1)
    reps = LANES // period
    c, a, b = (jnp.tile(t, (1, reps)) for t in (c, a, b))
    return c, a, b


def _apply_rope(x, c, a, b, shift):
    return x * c + pltpu.roll(x, LANES - shift, 1) * a + pltpu.roll(x, shift, 1) * b


def _proj_kernel(*refs, kind, shift):
    if kind == "rope":
        x_ref, w_ref, c_ref, a_ref, b_ref, o_ref = refs
    elif kind == "rmsnorm":
        x_ref, w_ref, g_ref, o_ref = refs
    else:
        x_ref, w_ref, o_ref = refs
    acc = _dot(x_ref[...], w_ref[...])
    if kind == "rmsnorm":
        ms = jnp.mean(acc * acc, axis=-1, keepdims=True)
        o_ref[...] = (acc * lax.rsqrt(ms + NORM_EPS) * g_ref[...]).astype(o_ref.dtype)
    elif kind == "sigmoid":
        o_ref[...] = jax.nn.sigmoid(acc).astype(o_ref.dtype)
    elif kind == "rope":
        c, a, b = c_ref[...], a_ref[...], b_ref[...]
        for g in range(acc.shape[1] // LANES):
            xg = acc[:, g * LANES:(g + 1) * LANES]
            o_ref[:, g * LANES:(g + 1) * LANES] = _apply_rope(xg, c, a, b, shift).astype(o_ref.dtype)
    else:
        o_ref[...] = acc.astype(o_ref.dtype)


def _proj(x, w, kind, out_dtype, tm, tn, tabs=None, g=None, shift=0, name="proj"):
    M, K = x.shape
    N = w.shape[1]
    in_specs = [pl.BlockSpec((tm, K), lambda i, j: (i, 0)),
                pl.BlockSpec((K, tn), lambda i, j: (0, j))]
    args = [x, w]
    if kind == "rope":
        in_specs += [pl.BlockSpec((tm, LANES), lambda i, j: (i, 0))] * 3
        args += list(tabs)
    elif kind == "rmsnorm":
        in_specs += [pl.BlockSpec((1, tn), lambda i, j: (0, j))]
        args += [g]
    return pl.pallas_call(
        functools.partial(_proj_kernel, kind=kind, shift=shift),
        out_shape=jax.ShapeDtypeStruct((M, N), out_dtype),
        grid=(M // tm, N // tn),
        in_specs=in_specs,
        out_specs=pl.BlockSpec((tm, tn), lambda i, j: (i, j)),
        compiler_params=_cparams(("parallel", "arbitrary")),
        name=name,
    )(*args)


def _proj_heads_kernel(x_ref, w_ref, c_ref, a_ref, b_ref, o_ref, *, shift, n_scaled, qscale):
    acc = _dot(x_ref[...], w_ref[...])
    c, a, b = c_ref[...], a_ref[...], b_ref[...]
    sc = jnp.where(pl.program_id(1) < n_scaled, qscale, 1.0).astype(F32)
    for g in range(o_ref.shape[0]):
        xg = acc[:, g * LANES:(g + 1) * LANES]
        o_ref[g] = (_apply_rope(xg, c, a, b, shift) * sc).astype(o_ref.dtype)


def _proj_heads(x, w, tabs, tm, hb, shift, n_scaled_heads, qscale, name):
    M, K = x.shape
    H = w.shape[1] // LANES
    assert n_scaled_heads % hb == 0
    return pl.pallas_call(
        functools.partial(_proj_heads_kernel, shift=shift, n_scaled=n_scaled_heads // hb, qscale=qscale),
        out_shape=jax.ShapeDtypeStruct((H, M, LANES), BF16),
        grid=(M // tm, H // hb),
        in_specs=[pl.BlockSpec((tm, K), lambda i, j: (i, 0)),
                  pl.BlockSpec((K, hb * LANES), lambda i, j: (0, j)),
                  pl.BlockSpec((tm, LANES), lambda i, j: (i, 0)),
                  pl.BlockSpec((tm, LANES), lambda i, j: (i, 0)),
                  pl.BlockSpec((tm, LANES), lambda i, j: (i, 0))],
        out_specs=pl.BlockSpec((hb, tm, LANES), lambda i, j: (j, i, 0)),
        compiler_params=_cparams(("parallel", "arbitrary")),
        name=name,
    )(x, w, *tabs)


def _q_up_kernel(x_ref, w_ref, c_ref, a_ref, b_ref, o_ref, *, shift, qscale):
    x = x_ref[...].astype(BF16)
    c, a, b = c_ref[...] * qscale, a_ref[...] * qscale, b_ref[...] * qscale
    for h in range(o_ref.shape[0]):
        acc = _dot(x, w_ref[:, h * HEAD_PAD:(h + 1) * HEAD_PAD])
        o_ref[h, :, :LANES] = (acc[:, :LANES] * qscale).astype(o_ref.dtype)
        o_ref[h, :, LANES:] = _apply_rope(acc[:, LANES:], c, a, b, shift).astype(o_ref.dtype)


def _q_up(qn, R, w, tabs, tm, shift, qscale, name):
    M = qn.shape[0]
    H = w.shape[1] // HEAD_PAD
    return pl.pallas_call(
        functools.partial(_q_up_kernel, shift=shift, qscale=qscale),
        out_shape=jax.ShapeDtypeStruct((H, M, HEAD_PAD), BF16),
        grid=(M // tm,),
        in_specs=[pl.BlockSpec((tm, R), lambda i: (i, 0)),
                  pl.BlockSpec((R, H * HEAD_PAD), lambda i: (0, 0)),
                  pl.BlockSpec((tm, LANES), lambda i: (i, 0)),
                  pl.BlockSpec((tm, LANES), lambda i: (i, 0)),
                  pl.BlockSpec((tm, LANES), lambda i: (i, 0))],
        out_specs=pl.BlockSpec((H, tm, HEAD_PAD), lambda i: (0, i, 0)),
        compiler_params=_cparams(("parallel",)),
        name=name,
    )(qn, w, *tabs)


def _kv_up_kernel(c_ref, pe_ref, wk_ref, wv_ref, k_ref, v_ref):
    c = c_ref[...]
    pe = pe_ref[...]
    for h in range(k_ref.shape[0]):
        k_ref[h, :, :LANES] = _dot(c, wk_ref[:, h * LANES:(h + 1) * LANES]).astype(k_ref.dtype)
        k_ref[h, :, LANES:] = pe
        v_ref[h] = _dot(c, wv_ref[:, h * LANES:(h + 1) * LANES]).astype(v_ref.dtype)


def _kv_up(c2, pe2, wk, wv, tl, name):
    R_, R = c2.shape
    H = wk.shape[1] // LANES
    return pl.pallas_call(
        _kv_up_kernel,
        out_shape=(jax.ShapeDtypeStruct((H, R_, HEAD_PAD), BF16),
                   jax.ShapeDtypeStruct((H, R_, LANES), BF16)),
        grid=(R_ // tl,),
        in_specs=[pl.BlockSpec((tl, R), lambda i: (i, 0)),
                  pl.BlockSpec((tl, LANES), lambda i: (i, 0)),
                  pl.BlockSpec((R, H * LANES), lambda i: (0, 0)),
                  pl.BlockSpec((R, H * LANES), lambda i: (0, 0))],
        out_specs=(pl.BlockSpec((H, tl, HEAD_PAD), lambda i: (0, i, 0)),
                   pl.BlockSpec((H, tl, LANES), lambda i: (0, i, 0))),
        compiler_params=_cparams(("parallel",)),
        name=name,
    )(c2, pe2, wk, wv)


def _visible_limit(qpos):
    return ((qpos >> CHUNK_SHIFT) + 1) << CHUNK_SHIFT


def _vis_mask(qpos0, tq, k0, tk, kv_len):
    qp = qpos0 + lax.broadcasted_iota(jnp.int32, (tq, 1), 0)
    lim = jnp.minimum(_visible_limit(qp), kv_len) - k0
    return lax.broadcasted_iota(jnp.int32, (tq, tk), 1) < lim


def _for_tiles(lo, hi, tk, big, fn):
    n = hi - lo
    n_big = n // big

    def big_body(g, carry):
        fn(pl.multiple_of((lo + g * big) * tk, tk), big * tk)
        return carry

    lax.fori_loop(0, n_big, big_body, 0)
    pos = lo + n_big * big
    size = big // 2
    while size >= 1:
        take = (n & size) != 0

        @pl.when(take)
        def _(pos=pos, size=size):
            fn(pl.multiple_of(pos * tk, tk), size * tk)

        pos = pos + jnp.where(take, size, 0)
        size //= 2


def _attend(score_tile, v_tile, n_full, n_vis, tk, big, s_ref, mx_ref, acc_ref):
    mx_ref[...] = jnp.full(mx_ref.shape, MASK_NEG, F32)

    def pass1(masked):
        def fn(k0, size):
            s = score_tile(k0, size, masked)
            s_ref[:, pl.ds(k0, size)] = s
            m = mx_ref[...]
            for g in range(size // LANES):
                m = jnp.maximum(m, s[:, g * LANES:(g + 1) * LANES])
            mx_ref[...] = m
        return fn

    _for_tiles(0, n_full, tk, big, pass1(False))
    _for_tiles(n_full, n_vis, tk, 1, pass1(True))
    mx_ref[...] = jnp.broadcast_to(jnp.max(mx_ref[...], axis=-1, keepdims=True), mx_ref.shape)
    acc_ref[...] = jnp.zeros(acc_ref.shape, F32)

    def pass2(k0, size):
        m = mx_ref[...]
        s = s_ref[:, pl.ds(k0, size)]
        p = jnp.concatenate([jnp.exp2(s[:, g * LANES:(g + 1) * LANES] - m) for g in range(size // LANES)],
                            axis=1)
        ones_col = (lax.broadcasted_iota(jnp.int32, (size, LANES), 1) == 0).astype(BF16)
        v_ext = jnp.concatenate([v_tile(k0, size), ones_col], axis=1)
        acc_ref[...] += _dot(p.astype(BF16), v_ext)

    _for_tiles(0, n_vis, tk, big, pass2)
    acc = acc_ref[...]
    return acc[:, :LANES] / acc[:, LANES:LANES + 1]


def _mla_kernel(q_ref, k_ref, v_ref, o_ref, s_ref, mx_ref, acc_ref, *, tq, tk, big, q_off, kv_len):
    qi = pl.program_id(2)
    qpos0 = q_off + qi * tq
    q = q_ref[0, 0]
    n_full = jnp.minimum(_visible_limit(qpos0), kv_len) // tk
    n_vis = (jnp.minimum(_visible_limit(qpos0 + tq - 1), kv_len) + tk - 1) // tk

    def score_tile(k0, size, masked):
        s = _dot_nt(q, k_ref[0, 0, pl.ds(k0, size), :])
        if masked:
            s = jnp.where(_vis_mask(qpos0, tq, k0, size, kv_len), s, MASK_NEG)
        return s

    out = _attend(score_tile, lambda k0, size: v_ref[0, 0, pl.ds(k0, size), :], n_full, n_vis, tk, big,
                  s_ref, mx_ref, acc_ref)
    o_ref[0] = out.astype(o_ref.dtype)


def _mla_attn(q, k, v, tq, tk, big, q_off, kv_len, name):
    H, B, Lq, _ = q.shape
    Lk = k.shape[2]
    return pl.pallas_call(
        functools.partial(_mla_kernel, tq=tq, tk=tk, big=big, q_off=q_off, kv_len=kv_len),
        out_shape=jax.ShapeDtypeStruct((B, Lq, H * MLA_V), BF16),
        grid=(B, H, Lq // tq),
        in_specs=[pl.BlockSpec((1, 1, tq, HEAD_PAD), lambda b, h, i: (h, b, i, 0)),
                  pl.BlockSpec((1, 1, Lk, HEAD_PAD), lambda b, h, i: (h, b, 0, 0)),
                  pl.BlockSpec((1, 1, Lk, MLA_V), lambda b, h, i: (h, b, 0, 0))],
        out_specs=pl.BlockSpec((1, tq, MLA_V), lambda b, h, i: (b, i, h)),
        scratch_shapes=[pltpu.VMEM((tq, Lk), F32), pltpu.VMEM((tq, LANES), F32),
                        pltpu.VMEM((tq, 2 * LANES), F32)],
        compiler_params=_cparams(("parallel", "parallel", "arbitrary")),
        name=name,
    )(q, k, v)


def _sortable_key(x):
    i = pltpu.bitcast(x + 0.0, jnp.int32)
    return i ^ ((i >> 31) & 0x7FFFFFFF)


def _dsa_kernel(qi_ref, w_ref, ki_ref, q_ref, k_ref, v_ref, o_ref,
                key_ref, bias_ref, s_ref, mx_ref, acc_ref,
                *, tq, tk, big, q_off, kv_len, topk, n_idx_heads, n_groups, rep, w_lane0):
    i_q = pl.program_id(1)
    qpos0 = q_off + i_q * tq
    Lk = key_ref.shape[1]
    n_vis = (jnp.minimum(_visible_limit(qpos0 + tq - 1), kv_len) + tk - 1) // tk
    neg_key = NEG_INF_KEY

    q_idx = qi_ref[:, 0].reshape(n_idx_heads * tq, LANES)
    w = w_ref[0]

    def score_body(j, carry):
        k0 = pl.multiple_of(j * tk, tk)
        kt = ki_ref[0, pl.ds(k0, tk), :]
        r = _dot_nt(q_idx, kt)
        sc = jnp.zeros((tq, tk), F32)
        for h in range(n_idx_heads):
            sc = sc + w[:, w_lane0 + h:w_lane0 + h + 1] * jnp.maximum(r[h * tq:(h + 1) * tq], 0.0)
        sc = jnp.where(_vis_mask(qpos0, tq, k0, tk, kv_len), sc, -jnp.inf)
        key_ref[:, pl.ds(k0, tk)] = _sortable_key(sc)
        return carry

    lax.fori_loop(0, n_vis, score_body, 0)

    def count(pred_fn):
        def body(j, acc):
            k0 = pl.multiple_of(j * tk, tk)
            keys = key_ref[:, pl.ds(k0, tk)]
            idx = k0 + lax.broadcasted_iota(jnp.int32, (tq, tk), 1)
            hit = pred_fn(keys, idx).astype(jnp.int32)
            for g in range(tk // LANES):
                acc = acc + hit[:, g * LANES:(g + 1) * LANES]
            return acc
        acc = lax.fori_loop(0, n_vis, body, jnp.zeros((tq, LANES), jnp.int32))
        return jnp.sum(acc, axis=-1, keepdims=True)

    def bit_body(bi, t):
        cand = t | lax.shift_left(jnp.int32(1), 31 - bi)
        cs = cand ^ INT_MIN
        cnt = count(lambda keys, idx: keys >= cs)
        return jnp.where(cnt >= topk, cand, t)

    t_u = lax.fori_loop(0, 32, bit_body, jnp.zeros((tq, 1), jnp.int32))
    vk = t_u ^ INT_MIN
    c_gt = count(lambda keys, idx: keys > vk)
    c_eq = count(lambda keys, idx: keys == vk)
    need = topk - c_gt

    idx_bits = max(1, (Lk - 1).bit_length())
    tie_rows = jnp.logical_and(c_eq > need, vk > neg_key)
    any_tie = jnp.max(tie_rows.astype(jnp.int32)) > 0

    def tie_search():
        def jbit(bi, jv):
            cand = jv | lax.shift_left(jnp.int32(1), idx_bits - 1 - bi)
            cnt = count(lambda keys, idx: jnp.logical_and(keys == vk, idx < cand))
            return jnp.where(cnt < need, cand, jv)
        return lax.fori_loop(0, idx_bits, jbit, jnp.zeros((tq, 1), jnp.int32))

    j_lim = lax.cond(any_tie, tie_search, lambda: jnp.full((tq, 1), 2 ** 30, jnp.int32))

    def bias_body(j, carry):
        k0 = pl.multiple_of(j * tk, tk)
        keys = key_ref[:, pl.ds(k0, tk)]
        idx = k0 + lax.broadcasted_iota(jnp.int32, (tq, tk), 1)
        sel = jnp.logical_or(keys > vk, jnp.logical_and(keys == vk, idx <= j_lim))
        sel = jnp.logical_and(sel, keys > neg_key)
        bias_ref[:, pl.ds(k0, tk)] = jnp.where(sel, 0.0, MASK_NEG).astype(F32)
        return carry

    lax.fori_loop(0, n_vis, bias_body, 0)

    for g in range(n_groups):
        qg = q_ref[g * rep:(g + 1) * rep, 0].reshape(rep * tq, LANES)

        def score_tile(k0, size, masked):
            s = _dot_nt(qg, k_ref[0, pl.ds(k0, size), g * LANES:(g + 1) * LANES])
            return (s.reshape(rep, tq, size) + bias_ref[:, pl.ds(k0, size)][None]).reshape(rep * tq, size)

        out = _attend(score_tile, lambda k0, size: v_ref[0, pl.ds(k0, size), g * LANES:(g + 1) * LANES],
                      n_vis, n_vis, tk, big, s_ref, mx_ref, acc_ref)
        for r_ in range(rep):
            h = g * rep + r_
            o_ref[0, :, h * LANES:(h + 1) * LANES] = out[r_ * tq:(r_ + 1) * tq].astype(o_ref.dtype)


def _dsa_attn(qheads, n_heads, w_tail, k_idx, k, v, tq, tk, big, q_off, kv_len, topk, w_lane0, name):
    _, B, Lq, _ = qheads.shape
    H = n_heads
    Hi = qheads.shape[0] - H
    assert H == Hi, "attention and indexer head blocks are addressed as equal halves"
    Lk = k.shape[1]
    G = k.shape[2] // LANES
    rep = H // G
    kern = functools.partial(
        _dsa_kernel, tq=tq, tk=tk, big=big, q_off=q_off, kv_len=kv_len, topk=topk, n_idx_heads=Hi,
        n_groups=G, rep=rep, w_lane0=w_lane0)
    return pl.pallas_call(
        kern,
        out_shape=jax.ShapeDtypeStruct((B, Lq, H * LANES), BF16),
        grid=(B, Lq // tq),
        in_specs=[pl.BlockSpec((Hi, 1, tq, LANES), lambda b, i: (1, b, i, 0)),
                  pl.BlockSpec((1, tq, LANES), lambda b, i: (b, i, 0)),
                  pl.BlockSpec((1, Lk, LANES), lambda b, i: (b, 0, 0)),
                  pl.BlockSpec((H, 1, tq, LANES), lambda b, i: (0, b, i, 0)),
                  pl.BlockSpec((1, Lk, G * LANES), lambda b, i: (b, 0, 0)),
                  pl.BlockSpec((1, Lk, G * LANES), lambda b, i: (b, 0, 0))],
        out_specs=pl.BlockSpec((1, tq, H * LANES), lambda b, i: (b, i, 0)),
        scratch_shapes=[pltpu.VMEM((tq, Lk), jnp.int32), pltpu.VMEM((tq, Lk), F32),
                        pltpu.VMEM((rep * tq, Lk), F32), pltpu.VMEM((rep * tq, LANES), F32),
                        pltpu.VMEM((rep * tq, 2 * LANES), F32)],
        compiler_params=_cparams(("parallel", "arbitrary")),
        name=name,
    )(qheads, w_tail, k_idx, qheads, k, v)


def _merge_kernel(a_ref, b_ref, ga_ref, gb_ref, wa_ref, wb_ref, o_ref):
    ya = _dot(a_ref[...], wa_ref[...])
    yb = _dot(b_ref[...], wb_ref[...])
    o_ref[...] = (ga_ref[...].astype(F32) * ya + gb_ref[...].astype(F32) * yb).astype(o_ref.dtype)


def _merge(attn_a, attn_b, gates, wa, wb, tm, tn, name):
    M, Ka = attn_a.shape
    Kb = attn_b.shape[1]
    N = wa.shape[1]
    nb = N // tn
    return pl.pallas_call(
        _merge_kernel,
        out_shape=jax.ShapeDtypeStruct((M, N), BF16),
        grid=(M // tm, nb),
        in_specs=[pl.BlockSpec((tm, Ka), lambda i, j: (i, 0)),
                  pl.BlockSpec((tm, Kb), lambda i, j: (i, 0)),
                  pl.BlockSpec((tm, tn), lambda i, j: (i, j)),
                  pl.BlockSpec((tm, tn), lambda i, j: (i, j + nb)),
                  pl.BlockSpec((Ka, tn), lambda i, j: (0, j)),
                  pl.BlockSpec((Kb, tn), lambda i, j: (0, j))],
        out_specs=pl.BlockSpec((tm, tn), lambda i, j: (i, j)),
        compiler_params=_cparams(("parallel", "arbitrary")),
        name=name,
    )(attn_a, attn_b, gates, gates, wa, wb)


def _layernorm(z, g, b):
    mu = jnp.mean(z, axis=-1, keepdims=True)
    zc = z - mu
    var = jnp.mean(zc * zc, axis=-1, keepdims=True)
    return zc * lax.rsqrt(var + NORM_EPS) * g + b


def _out_ln_kernel(mg_ref, x_ref, w_ref, g_ref, b_ref, o_ref, *, alpha):
    mix = _dot(mg_ref[...], w_ref[...])
    o_ref[...] = _layernorm(alpha * x_ref[...] + mix, g_ref[...], b_ref[...])


def _out_ln(merged, x, w, g, b, tm, alpha, name):
    M, D = x.shape
    return pl.pallas_call(
        functools.partial(_out_ln_kernel, alpha=alpha),
        out_shape=jax.ShapeDtypeStruct((M, D), F32),
        grid=(M // tm,),
        in_specs=[pl.BlockSpec((tm, D), lambda i: (i, 0)),
                  pl.BlockSpec((tm, D), lambda i: (i, 0)),
                  pl.BlockSpec((D, D), lambda i: (0, 0)),
                  pl.BlockSpec((1, D), lambda i: (0, 0)),
                  pl.BlockSpec((1, D), lambda i: (0, 0))],
        out_specs=pl.BlockSpec((tm, D), lambda i: (i, 0)),
        compiler_params=_cparams(("parallel",)),
        name=name,
    )(merged, x, w, g, b)


def _ffn_kernel(h_ref, wu_ref, wd_ref, g_ref, b_ref, o_ref, acc_ref, *, alpha):
    f = pl.program_id(1)

    @pl.when(f == 0)
    def _():
        acc_ref[...] = jnp.zeros(acc_ref.shape, F32)

    u = jnp.maximum(_dot(h_ref[...].astype(BF16), wu_ref[...]), 0.0)
    acc_ref[...] += _dot((u * u).astype(BF16), wd_ref[...])

    @pl.when(f == pl.num_programs(1) - 1)
    def _():
        o_ref[...] = _layernorm(alpha * h_ref[...] + acc_ref[...], g_ref[...], b_ref[...])


def _ffn(h, wu, wd, g, b, tm, tf, alpha, name):
    M, D = h.shape
    F = wu.shape[1]
    return pl.pallas_call(
        functools.partial(_ffn_kernel, alpha=alpha),
        out_shape=jax.ShapeDtypeStruct((M, D), F32),
        grid=(M // tm, F // tf),
        in_specs=[pl.BlockSpec((tm, D), lambda i, f: (i, 0)),
                  pl.BlockSpec((D, tf), lambda i, f: (0, f)),
                  pl.BlockSpec((tf, D), lambda i, f: (f, 0)),
                  pl.BlockSpec((1, D), lambda i, f: (0, 0)),
                  pl.BlockSpec((1, D), lambda i, f: (0, 0))],
        out_specs=pl.BlockSpec((tm, D), lambda i, f: (i, 0)),
        scratch_shapes=[pltpu.VMEM((tm, D), F32)],
        compiler_params=_cparams(("parallel", "arbitrary")),
        name=name,
    )(h, wu, wd, g, b)


def _prep_weights(w_in, g_q_norm, g_kv_norm, w_uq, w_ukv, w_o_mla, w_o_dsa, w_out, w_up, w_down, dims):
    (q_lora, kv_lora, mla_rope, dsa_q, dsa_kv, idx_q, idx_dim, idx_heads, d_model) = dims
    cuts = [q_lora, kv_lora, mla_rope, dsa_q, dsa_kv, dsa_kv, idx_q, idx_dim, idx_heads, d_model, d_model]
    offs = [0]
    for c in cuts:
        offs.append(offs[-1] + c)
    col = lambda i: w_in[:, offs[i]:offs[i + 1]]
    wb = lambda t: t.astype(BF16)
    tail_pad = LANES - mla_rope - idx_heads
    W = {
        "lat": wb(jnp.concatenate([col(0), col(1)], axis=1)),
        "qheads": wb(jnp.concatenate([col(3), col(6)], axis=1)),
        "krope": wb(jnp.concatenate([col(4), col(7)], axis=1)),
        "v": wb(col(5)),
        "gates": wb(jnp.concatenate([col(9), col(10)], axis=1)),
        "tail": wb(jnp.concatenate([col(2), col(8), jnp.zeros((w_in.shape[0], tail_pad), w_in.dtype)], axis=1)),
        "g_lat": jnp.concatenate([g_q_norm, g_kv_norm])[None, :].astype(F32),
    }
    H = w_uq.shape[1]
    pe = w_uq.shape[2] - MLA_NOPE
    wq = jnp.concatenate([w_uq, jnp.zeros((w_uq.shape[0], H, HEAD_PAD - MLA_NOPE - pe), w_uq.dtype)], axis=2)
    W["uq"] = wb(wq.reshape(w_uq.shape[0], H * HEAD_PAD))
    W["uk"] = wb(w_ukv[:, :, :MLA_NOPE].reshape(w_ukv.shape[0], H * MLA_NOPE))
    W["uv"] = wb(w_ukv[:, :, MLA_NOPE:].reshape(w_ukv.shape[0], H * MLA_V))
    W["o_mla"], W["o_dsa"], W["out"] = wb(w_o_mla), wb(w_o_dsa), wb(w_out)
    W["up"], W["down"] = wb(w_up), wb(w_down)
    return W


def _layer(x, q_off, past, W, ln, dims, alpha, tag):
    (q_lora, kv_lora, mla_rope, dsa_q, dsa_kv, idx_q, idx_dim, idx_heads, d_model) = dims
    B, L, D = x.shape
    M = B * L
    assert q_lora == kv_lora and mla_rope + idx_heads <= LANES
    x2 = x.reshape(M, D)
    xb = x2.astype(BF16)
    pos = q_off + jnp.arange(L, dtype=jnp.int32)
    tm = _pick_tile(M, 1024, 16)

    lane = jnp.arange(LANES)[None, :]
    t32 = tuple(jnp.tile(t, (B, 1)) for t in _rope_tables(pos, LANES // 4, LANES))
    t64 = tuple(jnp.tile(jnp.where(lane < mla_rope, t, 0.0).astype(F32), (B, 1))
                for t in _rope_tables(pos, mla_rope, mla_rope))
    idx_scale = float(idx_q) ** -0.5
    in_w = jnp.logical_and(lane >= mla_rope, lane < mla_rope + idx_heads)
    tail_tabs = (jnp.where(in_w, idx_scale, t64[0]).astype(F32), t64[1], t64[2])

    lat = _proj(xb, W["lat"], "rmsnorm", F32, tm, q_lora, g=W["g_lat"], name=f"proj_lat_{tag}")
    n_dsa_heads = dsa_q // LANES
    qheads = _proj_heads(xb, W["qheads"], t32, tm, 4, LANES // 8, n_dsa_heads, LOG2E * LANES ** -0.5,
                         name=f"proj_qheads_{tag}")
    krope = _proj(xb, W["krope"], "rope", F32, tm, W["krope"].shape[1], tabs=t32,
                  shift=LANES // 8, name=f"proj_krope_{tag}")
    v_b = _proj(xb, W["v"], "none", F32, tm, W["v"].shape[1], name=f"proj_v_{tag}")
    gates = _proj(xb, W["gates"], "sigmoid", F32, tm, _pick_tile(d_model, 512, LANES), name=f"proj_gates_{tag}")
    tail = _proj(xb, W["tail"], "rope", F32, tm, LANES, tabs=tail_tabs, shift=mla_rope // 2,
                 name=f"proj_tail_{tag}")

    c_kv = lat[:, q_lora:].reshape(B, L, kv_lora)
    k_pe = tail[:, :mla_rope].reshape(B, L, mla_rope)
    k_b = krope[:, :dsa_kv].reshape(B, L, dsa_kv)
    k_i = krope[:, dsa_kv:].reshape(B, L, idx_dim)
    v_b3 = v_b.reshape(B, L, dsa_kv)
    new_rows = (c_kv, k_pe, k_b, v_b3, k_i)

    tail3 = tail.reshape(B, L, LANES)
    pe_new = jnp.where(lane[None] < mla_rope, tail3, 0.0)
    if past is None:
        kv_len = L
        tk = _pick_tile(L, 512, LANES)
        Lk = L
        c_all, pe_all, kb_all, vb_all, ki_all = (t.astype(BF16) for t in (c_kv, pe_new, k_b, v_b3, k_i))
    else:
        p_lat, p_rope, p_k, p_v, p_ik = past
        P = p_lat.shape[1]
        kv_len = P + L
        tk = 256
        Lk = -(-kv_len // tk) * tk
        p_rope_pad = jnp.concatenate([p_rope, jnp.zeros((B, P, LANES - mla_rope), p_rope.dtype)], axis=2)

        def cat(p, n):
            z = jnp.zeros((B, Lk - kv_len, n.shape[2]), BF16)
            return jnp.concatenate([p.astype(BF16), n.astype(BF16), z], axis=1)

        c_all, pe_all = cat(p_lat, c_kv), cat(p_rope_pad, pe_new)
        kb_all, vb_all = cat(p_k.reshape(B, P, dsa_kv), k_b), cat(p_v.reshape(B, P, dsa_kv), v_b3)
        ki_all = cat(p_ik, k_i)

    big = max(1, 2048 // tk)

    H = W["uk"].shape[1] // MLA_NOPE
    q_cat = _q_up(lat, q_lora, W["uq"], t64, tm, mla_rope // 2, LOG2E * (MLA_NOPE + mla_rope) ** -0.5,
                  name=f"q_up_{tag}")
    k_cat, v_mla = _kv_up(c_all.reshape(B * Lk, kv_lora), pe_all.reshape(B * Lk, LANES), W["uk"], W["uv"],
                          _pick_tile(B * Lk, 512, 16), name=f"kv_up_{tag}")
    tq_a = _pick_tile(L, 1024, 16)
    attn_a = _mla_attn(q_cat.reshape(H, B, L, HEAD_PAD), k_cat.reshape(H, B, Lk, HEAD_PAD),
                       v_mla.reshape(H, B, Lk, MLA_V), tq_a, tk, big, q_off, kv_len, name=f"mla_attn_{tag}")

    topk = min(TOPK_MAX, kv_len // 4)
    tq_b = _pick_tile(L, 128, 16)
    attn_b = _dsa_attn(qheads.reshape(-1, B, L, LANES), n_dsa_heads, tail3, ki_all, kb_all, vb_all,
                       tq_b, tk, big, q_off, kv_len, topk, mla_rope, name=f"dsa_attn_{tag}")

    tn = _pick_tile(d_model, 512, LANES)
    merged = _merge(attn_a.reshape(M, -1), attn_b.reshape(M, -1), gates, W["o_mla"], W["o_dsa"],
                    tm, tn, name=f"merge_{tag}")
    tm_s = _pick_tile(M, 512, 16)
    h = _out_ln(merged, x2, W["out"], ln[0], ln[1], tm_s, alpha, name=f"out_ln_{tag}")
    y = _ffn(h, W["up"], W["down"], ln[2], ln[3], tm_s, _pick_tile(W["up"].shape[1], 512, LANES), alpha,
             name=f"ffn_{tag}")
    return y.reshape(B, L, D), new_rows


def kernel(x_prompt, x_sample, cache_mla_latent, cache_mla_rope, cache_dsa_k, cache_dsa_v, cache_dsa_idx_k, w_in, g_q_norm, g_kv_norm, w_uq, w_ukv, w_o_mla, w_o_dsa, w_out, ln1_g, ln1_b, w_up, w_down, ln2_g, ln2_b):
    depth = w_in.shape[0]
    alpha = (2 * depth) ** 0.25
    d_model = x_prompt.shape[2]
    q_lora = g_q_norm.shape[1]
    kv_lora = g_kv_norm.shape[1]
    mla_rope = cache_mla_rope.shape[3]
    G, dh = cache_dsa_k.shape[3], cache_dsa_k.shape[4]
    dsa_kv = G * dh
    dsa_q = w_o_dsa.shape[1]
    idx_dim = cache_dsa_idx_k.shape[3]
    fixed = q_lora + kv_lora + mla_rope + dsa_q + 2 * dsa_kv + idx_dim + 2 * d_model
    idx_heads = (w_in.shape[2] - fixed) // (idx_dim + 1)
    idx_q = idx_heads * idx_dim
    dims = (q_lora, kv_lora, mla_rope, dsa_q, dsa_kv, idx_q, idx_dim, idx_heads, d_model)
    past_len = cache_mla_latent.shape[2]

    hp, hs = x_prompt, x_sample
    rows_p, rows_s = [], []
    for layer in range(depth):
        W = _prep_weights(w_in[layer], g_q_norm[layer], g_kv_norm[layer], w_uq[layer], w_ukv[layer],
                          w_o_mla[layer], w_o_dsa[layer], w_out[layer], w_up[layer], w_down[layer], dims)
        ln = tuple(t[layer][None, :].astype(F32) for t in (ln1_g, ln1_b, ln2_g, ln2_b))
        past = (cache_mla_latent[layer], cache_mla_rope[layer], cache_dsa_k[layer], cache_dsa_v[layer],
                cache_dsa_idx_k[layer])
        hp, new_p = _layer(hp, 0, None, W, ln, dims, alpha, f"p{layer}")
        hs, new_s = _layer(hs, past_len, past, W, ln, dims, alpha, f"s{layer}")
        rows_p.append(new_p)
        rows_s.append(new_s)

    def stack(rows, i, shape4=None):
        t = jnp.stack([r[i] for r in rows])
        if shape4 is not None:
            t = t.reshape(t.shape[:3] + shape4)
        return t

    outs = [hp, hs]
    for rows in (rows_p, rows_s):
        outs += [stack(rows, 0), stack(rows, 1), stack(rows, 2, (G, dh)), stack(rows, 3, (G, dh)), stack(rows, 4)]
    return tuple(outs)
```
